```python
import jax
import jax.numpy as jnp
from jax import lax
import numpy as np


D_MODEL = 2048
BATCH = 2
SEQ = 4096
DEPTH = 4

N_MIXERS = 2
HEAD_DIM = 128
SB_HEADS = D_MODEL // HEAD_DIM
SB_Q_BLOCK = 128
NSA_HEADS = D_MODEL // HEAD_DIM
NSA_KV_GROUPS = 4
NSA_HEADS_PER_GROUP = NSA_HEADS // NSA_KV_GROUPS
NSA_KV_WIDTH = NSA_KV_GROUPS * HEAD_DIM
NSA_N_BRANCHES = 3
NSA_IN_WIDTH = D_MODEL + 6 * NSA_KV_WIDTH + NSA_N_BRANCHES * NSA_HEADS
CMP_LEN = 32
CMP_STRIDE = 16
CMP_HIDDEN = HEAD_DIM
SEL_LEN = 64
SEL_TOP_N = 16
SEL_Q_BLOCK = 64
WIN = 512
WIN_Q_BLOCK = 128
D_FF = 4 * D_MODEL
N_SB_LAYERS = (DEPTH + 1) // 2
N_NSA_LAYERS = DEPTH // 2
NORM_EPS = 1e-6
NEG_INF = -1e30
FORCE_SCORE = 1e4

kernel_name = 'hybrid_stickbreak_nsa_sqrelu'


def rms_norm(x, g):
    xf = x.astype(jnp.float32)
    y = xf * lax.rsqrt(jnp.mean(xf * xf, axis=-1, keepdims=True) + NORM_EPS)
    return (y * g.astype(jnp.float32)).astype(x.dtype)


def alibi_slopes(n_heads):
    return jnp.exp2(-8.0 * jnp.arange(1, n_heads + 1, dtype=jnp.float32) / n_heads)


def sqrelu_mlp(h, w1, w2):
    return jnp.square(jax.nn.relu(h @ w1)) @ w2


def stick_breaking_attention(h, w_qkv, w_out):
    B, T, _ = h.shape
    q, k, v = jnp.split(h @ w_qkv, 3, axis=-1)
    heads = lambda a: a.reshape(B, T, SB_HEADS, HEAD_DIM).transpose(0, 2, 1, 3)
    q, k, v = heads(q), heads(k), heads(v)
    scale = HEAD_DIM ** -0.5
    outs = []
    for i in range(T // SB_Q_BLOCK):
        t0 = i * SB_Q_BLOCK
        end = t0 + SB_Q_BLOCK
        z = jnp.einsum('bhqd,bhsd->bhqs', q[:, :, t0:end], k[:, :, :end]).astype(jnp.float32) * scale
        causal = jnp.arange(end)[None, :] < jnp.arange(t0, end)[:, None]
        log_one_minus = jnp.where(causal, -jax.nn.softplus(z), 0.0)
        after = lax.cumsum(log_one_minus, axis=3, reverse=True) - log_one_minus
        a = jnp.where(causal, jnp.exp(jax.nn.log_sigmoid(z) + after), 0.0)
        outs.append(jnp.einsum('bhqs,bhsd->bhqd', a.astype(v.dtype), v[:, :, :end]))
    o = jnp.concatenate(outs, axis=2)
    return o.transpose(0, 2, 1, 3).reshape(B, T, D_MODEL) @ w_out


def compress_blocks(x, pe, w1, w2):
    B, G, T, Dh = x.shape
    n_cmp = (T - CMP_LEN) // CMP_STRIDE + 1
    idx = jnp.arange(n_cmp)[:, None] * CMP_STRIDE + jnp.arange(CMP_LEN)[None, :]
    blk = x[:, :, idx] + pe.astype(x.dtype)
    flat = blk.reshape(B, G, n_cmp, CMP_LEN * Dh)
    return jax.nn.gelu(flat @ w1) @ w2


def selected_attention(q, k, v, sel_idx, slopes):
    B, G, Hg, T, Dh = q.shape
    n = sel_idx.shape[-1]
    nqb = T // SEL_Q_BLOCK
    k_blk = k.reshape(B, G, T // SEL_LEN, SEL_LEN, Dh)
    v_blk = v.reshape(B, G, T // SEL_LEN, SEL_LEN, Dh)
    b_i = jnp.arange(B)[:, None, None, None]
    g_i = jnp.arange(G)[None, :, None, None]
    sl = slopes[None, :, :, None, None]
    scale = Dh ** -0.5

    def one_block(args):
        qb, ib, t0 = args
        kg = k_blk[b_i, g_i, ib].reshape(B, G, SEL_Q_BLOCK, n * SEL_LEN, Dh)
        vg = v_blk[b_i, g_i, ib].reshape(B, G, SEL_Q_BLOCK, n * SEL_LEN, Dh)
        t = t0 + jnp.arange(SEL_Q_BLOCK)
        pos = (ib[..., None] * SEL_LEN + jnp.arange(SEL_LEN)).reshape(B, G, SEL_Q_BLOCK, n * SEL_LEN)
        dist = (t[None, None, :, None] - pos)[:, :, None]
        s = jnp.einsum('bghqd,bgqkd->bghqk', qb, kg).astype(jnp.float32) * scale - sl * dist.astype(jnp.float32)
        p = jax.nn.softmax(jnp.where(dist >= 0, s, NEG_INF), axis=-1)
        return jnp.einsum('bghqk,bgqkd->bghqd', p.astype(qb.dtype), vg)

    q_blocks = q.reshape(B, G, Hg, nqb, SEL_Q_BLOCK, Dh).transpose(3, 0, 1, 2, 4, 5)
    i_blocks = sel_idx.reshape(B, G, nqb, SEL_Q_BLOCK, n).transpose(2, 0, 1, 3, 4)
    t0s = jnp.arange(nqb) * SEL_Q_BLOCK
    o = lax.map(one_block, (q_blocks, i_blocks, t0s))
    return o.transpose(1, 2, 3, 0, 4, 5).reshape(B, G, Hg, T, Dh)


def window_attention(q, k, v, slopes):
    B, G, Hg, T, Dh = q.shape
    nb = T // WIN_Q_BLOCK
    span = WIN_Q_BLOCK + WIN
    pad = ((0, 0), (0, 0), (WIN, 0), (0, 0))
    k_pad = jnp.pad(k, pad)
    v_pad = jnp.pad(v, pad)
    sl = slopes[None, :, :, None, None]
    scale = Dh ** -0.5

    def one_block(args):
        qb, t0 = args
        kb = lax.dynamic_slice_in_dim(k_pad, t0, span, axis=2)
        vb = lax.dynamic_slice_in_dim(v_pad, t0, span, axis=2)
        t = t0 + jnp.arange(WIN_Q_BLOCK)
        spos = t0 - WIN + jnp.arange(span)
        dist = t[:, None] - spos[None, :]
        mask = (dist >= 0) & (dist < WIN) & (spos[None, :] >= 0)
        s = jnp.einsum('bghqd,bgkd->bghqk', qb, kb).astype(jnp.float32) * scale - sl * dist.astype(jnp.float32)
        p = jax.nn.softmax(jnp.where(mask, s, NEG_INF), axis=-1)
        return jnp.einsum('bghqk,bgkd->bghqd', p.astype(qb.dtype), vb)

    q_blocks = q.reshape(B, G, Hg, nb, WIN_Q_BLOCK, Dh).transpose(3, 0, 1, 2, 4, 5)
    t0s = jnp.arange(nb) * WIN_Q_BLOCK
    o = lax.map(one_block, (q_blocks, t0s))
    return o.transpose(1, 2, 3, 0, 4, 5).reshape(B, G, Hg, T, Dh)


def native_sparse_attention(h, w_in, gate_b, q_norm_g, k_norm_g, cmp_pe, cmp_w1, cmp_w2, w_out):
    B, T, _ = h.shape
    G, Hg, Dh = NSA_KV_GROUPS, NSA_HEADS_PER_GROUP, HEAD_DIM
    offs = [D_MODEL + j * NSA_KV_WIDTH for j in range(7)]
    q, kc, vc, ks, vs, kw, vw, gl = jnp.split(h @ w_in, offs, axis=-1)
    q = rms_norm(q.reshape(B, T, G, Hg, Dh), q_norm_g).transpose(0, 2, 3, 1, 4)
    kv_heads = lambda a: a.reshape(B, T, G, Dh).transpose(0, 2, 1, 3)
    slopes = alibi_slopes(NSA_HEADS).reshape(G, Hg)
    scale = Dh ** -0.5

    kc_blk = rms_norm(compress_blocks(kv_heads(kc), cmp_pe[0], cmp_w1[0], cmp_w2[0]), k_norm_g[0])
    vc_blk = compress_blocks(kv_heads(vc), cmp_pe[1], cmp_w1[1], cmp_w2[1])
    n_cmp = kc_blk.shape[2]
    t_pos = jnp.arange(T)
    c_start = jnp.arange(n_cmp) * CMP_STRIDE
    dist_c = t_pos[:, None] - (c_start + CMP_LEN - 1)[None, :]
    mask_c = dist_c >= 0
    s_c = jnp.einsum('bghtd,bgnd->bghtn', q, kc_blk).astype(jnp.float32) * scale - slopes[None, :, :, None, None] * dist_c.astype(jnp.float32)
    p_cmp = jax.nn.softmax(jnp.where(mask_c, s_c, NEG_INF), axis=-1) * mask_c
    o_cmp = jnp.einsum('bghtn,bgnd->bghtd', p_cmp.astype(vc_blk.dtype), vc_blk)

    n_sel_blk = T // SEL_LEN
    s_start = jnp.arange(n_sel_blk) * SEL_LEN
    overlap = jnp.clip(jnp.minimum(c_start[:, None] + CMP_LEN, s_start[None, :] + SEL_LEN) - jnp.maximum(c_start[:, None], s_start[None, :]), 0, None).astype(jnp.float32) / CMP_LEN
    imp = jnp.einsum('bgtn,nj->bgtj', p_cmp.sum(axis=2), overlap)
    j = jnp.arange(n_sel_blk)[None, :]
    cur = (t_pos // SEL_LEN)[:, None]
    forced = (j == 0) | (j == cur) | (j == cur - 1)
    imp = jnp.where(j > cur, -1.0, jnp.where(forced, FORCE_SCORE, imp))
    _, sel_idx = lax.top_k(imp, min(SEL_TOP_N, n_sel_blk))

    o_slc = selected_attention(q, rms_norm(kv_heads(ks), k_norm_g[1]), kv_heads(vs), sel_idx, slopes)
    o_win = window_attention(q, rms_norm(kv_heads(kw), k_norm_g[2]), kv_heads(vw), slopes)

    gates = jax.nn.sigmoid((gl + gate_b).astype(jnp.float32)).astype(h.dtype)
    gates = gates.reshape(B, T, NSA_N_BRANCHES, G, Hg).transpose(2, 0, 3, 4, 1)[..., None]
    o = gates[0] * o_cmp + gates[1] * o_slc + gates[2] * o_win
    return o.transpose(0, 3, 1, 2, 4).reshape(B, T, D_MODEL) @ w_out


def setup_inputs(seed: int = 0) -> dict:
    key = jax.random.key(seed)
    ks = jax.random.split(key, 16)
    nrm = lambda k, shape: jax.random.normal(k, shape, jnp.float32)
    dense = lambda k, shape, fan_in: nrm(k, shape) * (fan_in ** -0.5)
    gain = lambda k, shape: 1.0 + 0.02 * nrm(k, shape)
    return {
        'x': nrm(ks[0], (BATCH, SEQ, D_MODEL)),
        'sb_norm_g': gain(ks[1], (N_SB_LAYERS, D_MODEL)),
        'sb_w_qkv': dense(ks[2], (N_SB_LAYERS, D_MODEL, 3 * D_MODEL), D_MODEL),
        'sb_w_out': dense(ks[3], (N_SB_LAYERS, D_MODEL, D_MODEL), D_MODEL),
        'nsa_norm_g': gain(ks[4], (N_NSA_LAYERS, D_MODEL)),
        'nsa_w_in': dense(ks[5], (N_NSA_LAYERS, D_MODEL, NSA_IN_WIDTH), D_MODEL),
        'nsa_gate_b': 0.02 * nrm(ks[6], (N_NSA_LAYERS, NSA_N_BRANCHES * NSA_HEADS)),
        'nsa_q_norm_g': gain(ks[7], (N_NSA_LAYERS, HEAD_DIM)),
        'nsa_k_norm_g': gain(ks[8], (N_NSA_LAYERS, NSA_N_BRANCHES, HEAD_DIM)),
        'nsa_cmp_pe': 0.1 * nrm(ks[9], (N_NSA_LAYERS, 2, CMP_LEN, HEAD_DIM)),
        'nsa_cmp_w1': dense(ks[10], (N_NSA_LAYERS, 2, CMP_LEN * HEAD_DIM, CMP_HIDDEN), CMP_LEN * HEAD_DIM),
        'nsa_cmp_w2': dense(ks[11], (N_NSA_LAYERS, 2, CMP_HIDDEN, HEAD_DIM), CMP_HIDDEN),
        'nsa_w_out': dense(ks[12], (N_NSA_LAYERS, D_MODEL, D_MODEL), D_MODEL),
        'mlp_norm_g': gain(ks[13], (DEPTH, D_MODEL)),
        'mlp_w1': dense(ks[14], (DEPTH, D_MODEL, D_FF), D_MODEL),
        'mlp_w2': dense(ks[15], (DEPTH, D_FF, D_MODEL), D_FF),
    }


def reference(x, sb_norm_g, sb_w_qkv, sb_w_out, nsa_norm_g, nsa_w_in, nsa_gate_b, nsa_q_norm_g, nsa_k_norm_g, nsa_cmp_pe, nsa_cmp_w1, nsa_cmp_w2, nsa_w_out, mlp_norm_g, mlp_w1, mlp_w2):
    h = x
    for layer in range(DEPTH):
        slot = layer // N_MIXERS
        if layer % N_MIXERS == 0:
            h = h + stick_breaking_attention(rms_norm(h, sb_norm_g[slot]), sb_w_qkv[slot], sb_w_out[slot])
        else:
            h = h + native_sparse_attention(rms_norm(h, nsa_norm_g[slot]), nsa_w_in[slot], nsa_gate_b[slot], nsa_q_norm_g[slot], nsa_k_norm_g[slot], nsa_cmp_pe[slot], nsa_cmp_w1[slot], nsa_cmp_w2[slot], nsa_w_out[slot])
        h = h + sqrelu_mlp(rms_norm(h, mlp_norm_g[layer]), mlp_w1[layer], mlp_w2[layer])
    return h
```

```python
import functools

import numpy as np
import jax
import jax.numpy as jnp
from jax import lax
from jax.experimental import pallas as pl
from jax.experimental.pallas import tpu as pltpu

HEAD_DIM = 128
KV_GROUPS = 4
N_BRANCHES = 3
CMP_LEN = 32
CMP_STRIDE = 16
SEL_LEN = 64
SEL_TOP_N = 16
WIN = 512
NORM_EPS = 1e-6
NEG_INF = -1e30
FORCE_SCORE = 1e4

LANES = 128
VMEM_LIMIT = 56 * 1024 * 1024
MXU_DTYPE = jnp.bfloat16
F32 = jnp.float32

_NT = (((1,), (1,)), ((), ()))


def _params(sem):
    return pltpu.CompilerParams(dimension_semantics=sem, vmem_limit_bytes=VMEM_LIMIT)


def _rms(x, gain):
    ms = jnp.mean(x * x, axis=-1, keepdims=True)
    return x * lax.rsqrt(ms + NORM_EPS) * gain


def _matmul_body(flags_ref, x_ref, g_ref, w_ref, hg_ref, *rest, has_norm, has_res, tn):
    if has_res:
        r_ref, o_ref, *scratch = rest
    else:
        o_ref, *scratch = rest
    j = pl.program_id(1)
    if has_norm:
        xn_ref, = scratch

        @pl.when(j == 0)
        def _():
            xn_ref[...] = _rms(x_ref[...], g_ref[...]).astype(xn_ref.dtype)

        lhs = xn_ref[...]
    else:
        lhs = x_ref[...]
    acc = jnp.dot(lhs, w_ref[...], preferred_element_type=F32)
    if has_res:
        acc = acc + r_ref[...]
    flag = flags_ref[j]

    @pl.when(flag == 0)
    def _():
        o_ref[...] = acc.astype(o_ref.dtype)

    @pl.when(flag != 0)
    def _():
        for c in range(tn // HEAD_DIM):
            cols = slice(c * HEAD_DIM, (c + 1) * HEAD_DIM)
            o_ref[:, cols] = _rms(acc[:, cols], hg_ref[:, cols]).astype(o_ref.dtype)


def _matmul(x, w, *, norm_g=None, residual=None, head_gain=None, head_flags=None,
            out_dtype, tm, tn):
    n, k = x.shape
    m = w.shape[1]
    tm = min(tm, n)
    tn = min(tn, m)
    nj = m // tn
    assert n % tm == 0 and m % tn == 0 and tn % HEAD_DIM == 0
    has_norm = norm_g is not None
    has_res = residual is not None
    g = (norm_g if has_norm else jnp.ones((k,), F32)).reshape(1, k).astype(F32)
    hg = (head_gain if head_gain is not None else jnp.ones((m,), F32)).reshape(1, m).astype(F32)
    flags = (head_flags if head_flags is not None else jnp.zeros((nj,), jnp.int32)).astype(jnp.int32)
    in_specs = [
        pl.BlockSpec((tm, k), lambda i, j, f: (i, 0)),
        pl.BlockSpec((1, k), lambda i, j, f: (0, 0)),
        pl.BlockSpec((k, tn), lambda i, j, f: (0, j)),
        pl.BlockSpec((1, tn), lambda i, j, f: (0, j)),
    ]
    args = [x, g, w, hg]
    if has_res:
        in_specs.append(pl.BlockSpec((tm, tn), lambda i, j, f: (i, j)))
        args.append(residual)
    scratch = [pltpu.VMEM((tm, k), MXU_DTYPE)] if has_norm else []
    return pl.pallas_call(
        functools.partial(_matmul_body, has_norm=has_norm, has_res=has_res, tn=tn),
        grid_spec=pltpu.PrefetchScalarGridSpec(
            num_scalar_prefetch=1,
            grid=(n // tm, nj),
            in_specs=in_specs,
            out_specs=pl.BlockSpec((tm, tn), lambda i, j, f: (i, j)),
            scratch_shapes=scratch,
        ),
        out_shape=jax.ShapeDtypeStruct((n, m), out_dtype),
        compiler_params=_params(("parallel", "arbitrary")),
        name="norm_matmul",
    )(flags, *args)


def _mlp_body(x_ref, g_ref, w1_ref, w2_ref, o_ref, xn_ref):
    j = pl.program_id(1)

    @pl.when(j == 0)
    def _():
        x = x_ref[...]
        xn_ref[...] = _rms(x, g_ref[...]).astype(xn_ref.dtype)
        o_ref[...] = x

    h = jnp.dot(xn_ref[...], w1_ref[...], preferred_element_type=F32)
    h = jnp.square(jnp.maximum(h, 0.0)).astype(w2_ref.dtype)
    o_ref[...] += jnp.dot(h, w2_ref[...], preferred_element_type=F32)


def _mlp(x, norm_g, w1, w2, *, tm, tf):
    n, d = x.shape
    f = w1.shape[1]
    assert n % tm == 0 and f % tf == 0
    return pl.pallas_call(
        _mlp_body,
        grid=(n // tm, f // tf),
        in_specs=[
            pl.BlockSpec((tm, d), lambda i, j: (i, 0)),
            pl.BlockSpec((1, d), lambda i, j: (0, 0)),
            pl.BlockSpec((d, tf), lambda i, j: (0, j)),
            pl.BlockSpec((tf, d), lambda i, j: (j, 0)),
        ],
        out_specs=pl.BlockSpec((tm, d), lambda i, j: (i, 0)),
        out_shape=jax.ShapeDtypeStruct((n, d), F32),
        scratch_shapes=[pltpu.VMEM((tm, d), MXU_DTYPE)],
        compiler_params=_params(("parallel", "arbitrary")),
        name="sqrelu_mlp",
    )(x, norm_g.reshape(1, d).astype(F32), w1, w2)


def _sb_body(q_ref, k_ref, v_ref, u_ref, o_ref, acc_ref, *, tq, scale):
    qi = pl.program_id(2)
    q = q_ref[...]
    u2 = u_ref[...]

    def block(kb, carry, diag):
        start = pl.multiple_of(kb * tq, tq)
        k = k_ref[pl.ds(start, tq), :]
        v = v_ref[pl.ds(start, tq), :]
        z = lax.dot_general(q, k, _NT, preferred_element_type=F32) * scale
        softplus = jnp.maximum(z, 0.0) + jnp.log(1.0 + jnp.exp(-jnp.abs(z)))
        log_one_minus = -softplus
        if diag:
            row = lax.broadcasted_iota(jnp.int32, (tq, tq), 0)
            col = lax.broadcasted_iota(jnp.int32, (tq, tq), 1)
            causal = col < row
            log_one_minus = jnp.where(causal, log_one_minus, 0.0)
        hi = log_one_minus.astype(MXU_DTYPE)
        lo = (log_one_minus - hi.astype(F32)).astype(MXU_DTYPE)
        after = jnp.dot(jnp.concatenate([hi, lo], axis=1), u2, preferred_element_type=F32) + carry
        a = jnp.exp(z - softplus + after)
        if diag:
            a = jnp.where(causal, a, 0.0)
        acc_ref[...] += jnp.dot(a.astype(v.dtype), v, preferred_element_type=F32)
        return carry + jnp.sum(log_one_minus, axis=1, keepdims=True)

    acc_ref[...] = jnp.zeros_like(acc_ref)
    carry = block(qi, jnp.zeros((tq, 1), F32), True)
    lax.fori_loop(0, qi, lambda jj, c: block(qi - 1 - jj, c, False), carry)
    o_ref[...] = acc_ref[...].astype(o_ref.dtype)


def _sb_attention(qkv, batch, seq, *, tq):
    n, d3 = qkv.shape
    d = d3 // 3
    heads = d // HEAD_DIM
    nq = seq // tq
    tri = np.tril(np.ones((tq, tq), np.float32), -1)
    u2 = jnp.asarray(np.concatenate([tri, tri], axis=0), MXU_DTYPE)
    return pl.pallas_call(
        functools.partial(_sb_body, tq=tq, scale=HEAD_DIM ** -0.5),
        grid=(batch, heads, nq),
        in_specs=[
            pl.BlockSpec((tq, HEAD_DIM), lambda b, h, i: (b * nq + i, h)),
            pl.BlockSpec((seq, HEAD_DIM), lambda b, h, i: (b, heads + h)),
            pl.BlockSpec((seq, HEAD_DIM), lambda b, h, i: (b, 2 * heads + h)),
            pl.BlockSpec((2 * tq, tq), lambda b, h, i: (0, 0)),
        ],
        out_specs=pl.BlockSpec((tq, HEAD_DIM), lambda b, h, i: (b * nq + i, h)),
        out_shape=jax.ShapeDtypeStruct((n, d), MXU_DTYPE),
        scratch_shapes=[pltpu.VMEM((tq, HEAD_DIM), F32)],
        compiler_params=_params(("parallel", "parallel", "arbitrary")),
        name="stick_breaking_attention",
    )(qkv, qkv, qkv, u2)


def _compress_body(x_ref, pe_ref, w1_ref, w2_ref, kg_ref, o_ref, *, half):
    which = pl.program_id(0)
    x = x_ref[...].astype(F32)
    top = (x + pe_ref[0]).astype(w1_ref.dtype)
    bot = (x + pe_ref[1]).astype(w1_ref.dtype)
    h_top = jnp.dot(top, w1_ref[:half, :], preferred_element_type=F32)
    h_bot = jnp.dot(bot, w1_ref[half:, :], preferred_element_type=F32)
    n_chunks = x.shape[0]
    pre = h_top + pltpu.roll(h_bot, n_chunks - 1, 0)
    cdf = 0.5 * (1.0 + jnp.tanh(np.sqrt(2.0 / np.pi).astype(np.float32) * (pre + 0.044715 * (pre * pre * pre))))
    y = jnp.dot((pre * cdf).astype(w2_ref.dtype), w2_ref[...], preferred_element_type=F32)

    @pl.when(which == 0)
    def _():
        o_ref[...] = _rms(y, kg_ref[...]).astype(o_ref.dtype)

    @pl.when(which != 0)
    def _():
        o_ref[...] = y.astype(o_ref.dtype)


def _compress(xc, pe, w1, w2, k_gain):
    _, b, g, nch, width = xc.shape
    sq = pl.Squeezed()
    return pl.pallas_call(
        functools.partial(_compress_body, half=width),
        grid=(2, b, g),
        in_specs=[
            pl.BlockSpec((sq, sq, sq, nch, width), lambda w, bi, gi: (w, bi, gi, 0, 0)),
            pl.BlockSpec((sq, 2, 1, width), lambda w, bi, gi: (w, 0, 0, 0)),
            pl.BlockSpec((sq, 2 * width, HEAD_DIM), lambda w, bi, gi: (w, 0, 0)),
            pl.BlockSpec((sq, HEAD_DIM, HEAD_DIM), lambda w, bi, gi: (w, 0, 0)),
            pl.BlockSpec((1, HEAD_DIM), lambda w, bi, gi: (0, 0)),
        ],
        out_specs=pl.BlockSpec((sq, sq, sq, nch, HEAD_DIM), lambda w, bi, gi: (w, bi, gi, 0, 0)),
        out_shape=jax.ShapeDtypeStruct((2, b, g, nch, HEAD_DIM), MXU_DTYPE),
        compiler_params=_params(("parallel", "parallel", "parallel")),
        name="nsa_compress",
    )(xc, pe, w1, w2, k_gain)


def _split3(x):
    hi = x.astype(MXU_DTYPE)
    r = x - hi.astype(F32)
    mid = r.astype(MXU_DTYPE)
    lo = (r - mid.astype(F32)).astype(MXU_DTYPE)
    return hi, mid, lo


def _cmp_select_body(slopes_ref, q_ref, kc_ref, vc_ref, ov_ref, o_ref, sel_ref, *, tq, hg, n_sel, scale):
    g = pl.program_id(1)
    i = pl.program_id(2)
    kc = kc_ref[...]
    vc = vc_ref[...]
    n_cmp = kc.shape[0]
    t = i * tq + lax.broadcasted_iota(jnp.int32, (tq, 1), 0)
    c_end = lax.broadcasted_iota(jnp.int32, (1, n_cmp), 1) * CMP_STRIDE + (CMP_LEN - 1)
    dist = t - c_end
    mask = dist >= 0
    maskf = mask.astype(F32)
    distf = dist.astype(F32)
    psum = jnp.zeros((tq, n_cmp), F32)
    for hh in range(hg):
        cols = slice(hh * HEAD_DIM, (hh + 1) * HEAD_DIM)
        s = lax.dot_general(q_ref[:, cols], kc, _NT, preferred_element_type=F32) * scale
        s = s - slopes_ref[g * hg + hh] * distf
        s = jnp.where(mask, s, NEG_INF)
        e = jnp.exp(s - jnp.max(s, axis=-1, keepdims=True))
        p = e / jnp.sum(e, axis=-1, keepdims=True) * maskf
        o_ref[:, cols] = jnp.dot(p.astype(vc.dtype), vc, preferred_element_type=F32).astype(o_ref.dtype)
        psum = psum + p
    ov = ov_ref[...]
    imp = sum(jnp.dot(part, ov, preferred_element_type=F32) for part in _split3(psum))

    jb = lax.broadcasted_iota(jnp.int32, (1, n_sel), 1)
    cur = t // SEL_LEN
    forced = jnp.where(jb == 0, 1.0, jnp.where(jb == cur, 1.0, jnp.where(jb == cur - 1, 1.0, 0.0)))
    imp = jnp.where(jb > cur, -1.0, jnp.where(forced > 0.5, FORCE_SCORE, imp))
    rank = jnp.zeros((tq, n_sel), F32)
    for b in range(n_sel):
        cb = imp[:, b:b + 1]
        tie = jnp.where(jb > b, 1.0, 0.0)
        rank = rank + jnp.where(cb > imp, 1.0, jnp.where(cb == imp, tie, 0.0))
    sel_ref[...] = jnp.where(rank < float(min(SEL_TOP_N, n_sel)), 1.0, 0.0).astype(sel_ref.dtype)


def _overlap_matrix(n_cmp_pad, n_sel):
    c0 = np.arange(n_cmp_pad)[:, None] * CMP_STRIDE
    s0 = np.arange(n_sel)[None, :] * SEL_LEN
    ov = np.clip(np.minimum(c0 + CMP_LEN, s0 + SEL_LEN) - np.maximum(c0, s0), 0, None)
    return (ov.astype(np.float32) / CMP_LEN)


def _cmp_select(proj, cblk, slopes, batch, seq, *, tq):
    n = proj.shape[0]
    groups = cblk.shape[2]
    n_cmp = cblk.shape[3]
    hg = slopes.shape[0] // groups
    d = groups * hg * HEAD_DIM
    n_sel = seq // SEL_LEN
    nq = seq // tq
    ov = jnp.asarray(_overlap_matrix(n_cmp, n_sel), MXU_DTYPE)
    sq = pl.Squeezed()
    return pl.pallas_call(
        functools.partial(_cmp_select_body, tq=tq, hg=hg, n_sel=n_sel, scale=HEAD_DIM ** -0.5),
        grid=(batch, groups, nq),
        in_specs=[
            pl.BlockSpec(memory_space=pltpu.SMEM),
            pl.BlockSpec((tq, hg * HEAD_DIM), lambda b, g, i: (b * nq + i, g)),
            pl.BlockSpec((sq, sq, sq, n_cmp, HEAD_DIM), lambda b, g, i: (0, b, g, 0, 0)),
            pl.BlockSpec((sq, sq, sq, n_cmp, HEAD_DIM), lambda b, g, i: (1, b, g, 0, 0)),
            pl.BlockSpec((n_cmp, n_sel), lambda b, g, i: (0, 0)),
        ],
        out_specs=[
            pl.BlockSpec((tq, hg * HEAD_DIM), lambda b, g, i: (b * nq + i, g)),
            pl.BlockSpec((sq, sq, tq, n_sel), lambda b, g, i: (b, g, i, 0)),
        ],
        out_shape=[
            jax.ShapeDtypeStruct((n, d), MXU_DTYPE),
            jax.ShapeDtypeStruct((batch, groups, seq, n_sel), MXU_DTYPE),
        ],
        compiler_params=_params(("parallel", "parallel", "parallel")),
        name="nsa_cmp_select",
    )(slopes, proj, cblk, cblk, ov)


def _nsa_attn_body(slopes_ref, q_ref, ks_ref, vs_ref, kw_ref, vw_ref, sel_ref, exp_ref, ocmp_ref,
                   gl_ref, gb_ref, o_ref, m_ref, l_ref, acc_ref, out_ref, *, tq, hg, scale):
    g = pl.program_id(1)
    qi = pl.program_id(2)
    row = lax.broadcasted_iota(jnp.int32, (tq, tq), 0)
    col = lax.broadcasted_iota(jnp.int32, (tq, tq), 1)
    rel = (row - col).astype(F32)
    gates = jax.nn.sigmoid(gl_ref[...] + gb_ref[...])

    def reset():
        m_ref[...] = jnp.full_like(m_ref, NEG_INF)
        l_ref[...] = jnp.zeros_like(l_ref)
        acc_ref[...] = jnp.zeros_like(acc_ref)

    def tile(k_ref, v_ref, kb, valid, distf):
        start = pl.multiple_of(kb * tq, tq)
        k = k_ref[pl.ds(start, tq), :]
        v = v_ref[pl.ds(start, tq), :]
        for hh in range(hg):
            cols = slice(hh * HEAD_DIM, (hh + 1) * HEAD_DIM)
            s = lax.dot_general(q_ref[:, cols], k, _NT, preferred_element_type=F32) * scale
            s = s - slopes_ref[g * hg + hh] * distf
            if valid is not None:
                s = jnp.where(valid, s, NEG_INF)
            m_old = m_ref[hh]
            m_new = jnp.maximum(m_old, jnp.max(s, axis=-1, keepdims=True))
            alpha = jnp.exp(m_old - m_new)
            p = jnp.exp(s - m_new)
            l_ref[hh] = alpha * l_ref[hh] + jnp.sum(p, axis=-1, keepdims=True)
            acc_ref[hh] = alpha * acc_ref[hh] + jnp.dot(p.astype(v.dtype), v, preferred_element_type=F32)
            m_ref[hh] = m_new

    def finish(branch, first):
        for hh in range(hg):
            cols = slice(hh * HEAD_DIM, (hh + 1) * HEAD_DIM)
            gate = gates[:, branch * hg + hh:branch * hg + hh + 1]
            part = gate * (acc_ref[hh] / l_ref[hh])
            if first:
                gate0 = gates[:, hh:hh + 1]
                out_ref[:, cols] = gate0 * ocmp_ref[:, cols].astype(F32) + part
            else:
                out_ref[:, cols] += part

    sel = sel_ref[...]
    reset()
    picked = jnp.dot(sel, exp_ref[qi], preferred_element_type=F32)
    tile(ks_ref, vs_ref, qi, jnp.where(col <= row, picked, 0.0) > 0.5, rel)

    def sel_step(kb, _):
        picked = jnp.dot(sel, exp_ref[kb], preferred_element_type=F32) > 0.5
        tile(ks_ref, vs_ref, kb, picked, rel + ((qi - kb) * tq).astype(F32))
        return 0

    lax.fori_loop(0, qi, sel_step, 0)
    finish(1, True)

    reset()
    tile(kw_ref, vw_ref, qi, col <= row, rel)
    n_back = WIN // tq
    for back in range(1, n_back + 1):
        @pl.when(qi >= back)
        def _():
            valid = (col > row) if back == n_back else None
            tile(kw_ref, vw_ref, qi - back, valid, rel + float(back * tq))
    finish(2, False)
    o_ref[...] = out_ref[...].astype(o_ref.dtype)


def _nsa_attention(proj, sel, o_cmp, gl, gb, slopes, batch, seq, *, tq):
    n = proj.shape[0]
    groups = sel.shape[1]
    hg = slopes.shape[0] // groups
    d = groups * hg * HEAD_DIM
    n_sel = seq // SEL_LEN
    nq = seq // tq
    assert WIN % tq == 0 and tq % SEL_LEN == 0
    qb = d // HEAD_DIM
    blk_of_key = np.arange(seq) // SEL_LEN
    expand = (np.arange(n_sel)[:, None] == blk_of_key[None, :]).astype(np.float32)
    expand = jnp.asarray(expand.reshape(n_sel, nq, tq).transpose(1, 0, 2), MXU_DTYPE)
    sq = pl.Squeezed()
    kv_spec = lambda off: pl.BlockSpec((seq, HEAD_DIM), lambda b, g, i: (b, qb + off * groups + g))
    return pl.pallas_call(
        functools.partial(_nsa_attn_body, tq=tq, hg=hg, scale=HEAD_DIM ** -0.5),
        grid=(batch, groups, nq),
        in_specs=[
            pl.BlockSpec(memory_space=pltpu.SMEM),
            pl.BlockSpec((tq, hg * HEAD_DIM), lambda b, g, i: (b * nq + i, g)),
            kv_spec(2), kv_spec(3), kv_spec(4), kv_spec(5),
            pl.BlockSpec((sq, sq, tq, n_sel), lambda b, g, i: (b, g, i, 0)),
            pl.BlockSpec((nq, n_sel, tq), lambda b, g, i: (0, 0, 0)),
            pl.BlockSpec((tq, hg * HEAD_DIM), lambda b, g, i: (b * nq + i, g)),
            pl.BlockSpec((sq, sq, tq, N_BRANCHES * hg), lambda b, g, i: (b, g, i, 0)),
            pl.BlockSpec((sq, 1, N_BRANCHES * hg), lambda b, g, i: (g, 0, 0)),
        ],
        out_specs=pl.BlockSpec((tq, hg * HEAD_DIM), lambda b, g, i: (b * nq + i, g)),
        out_shape=jax.ShapeDtypeStruct((n, d), MXU_DTYPE),
        scratch_shapes=[
            pltpu.VMEM((hg, tq, 1), F32),
            pltpu.VMEM((hg, tq, 1), F32),
            pltpu.VMEM((hg, tq, HEAD_DIM), F32),
            pltpu.VMEM((tq, hg * HEAD_DIM), F32),
        ],
        compiler_params=_params(("parallel", "parallel", "arbitrary")),
        name="nsa_select_window_attention",
    )(slopes, proj, proj, proj, proj, proj, sel, expand, o_cmp, gl, gb)


def _sb_layer(h, batch, seq, norm_g, w_qkv, w_out):
    qkv = _matmul(h, w_qkv.astype(MXU_DTYPE), norm_g=norm_g, out_dtype=MXU_DTYPE, tm=1024, tn=512)
    o = _sb_attention(qkv, batch, seq, tq=256)
    return _matmul(o, w_out.astype(MXU_DTYPE), residual=h, out_dtype=F32, tm=1024, tn=512)


def _nsa_layer(h, batch, seq, norm_g, w_in, gate_b, q_norm_g, k_norm_g, cmp_pe, cmp_w1, cmp_w2, w_out):
    n, d = h.shape
    heads = d // HEAD_DIM
    groups = KV_GROUPS
    hg = heads // groups
    kvw = groups * HEAD_DIM
    main = d + 6 * kvw
    tn = 512
    ones = jnp.ones((kvw,), F32)
    head_gain = jnp.concatenate([jnp.tile(q_norm_g, heads), ones, ones, jnp.tile(k_norm_g[1], groups), ones,
                                 jnp.tile(k_norm_g[2], groups), ones])
    col_normed = np.concatenate([np.ones(d), np.zeros(2 * kvw), np.ones(kvw), np.zeros(kvw), np.ones(kvw),
                                 np.zeros(kvw)])
    head_flags = jnp.asarray(col_normed.reshape(main // tn, tn)[:, 0], jnp.int32)
    proj = _matmul(h, w_in[:, :main].astype(MXU_DTYPE), norm_g=norm_g, head_gain=head_gain,
                   head_flags=head_flags, out_dtype=MXU_DTYPE, tm=1024, tn=tn)
    n_gate = N_BRANCHES * heads
    w_gate = jnp.pad(w_in[:, main:], ((0, 0), (0, LANES - n_gate))).astype(MXU_DTYPE)
    gl = _matmul(h, w_gate, norm_g=norm_g, out_dtype=F32, tm=1024, tn=LANES)[:, :n_gate]
    gl = gl.reshape(batch, seq, N_BRANCHES, groups, hg).transpose(0, 3, 1, 2, 4).reshape(batch, groups, seq, N_BRANCHES * hg)
    gb = gate_b.reshape(N_BRANCHES, groups, hg).transpose(1, 0, 2).reshape(groups, 1, N_BRANCHES * hg)

    chunk = CMP_LEN // 2
    assert CMP_STRIDE == chunk
    xc = proj[:, d:d + 2 * kvw].reshape(batch, seq // chunk, chunk, 2, groups, HEAD_DIM)
    xc = xc.transpose(3, 0, 4, 1, 2, 5).reshape(2, batch, groups, seq // chunk, chunk * HEAD_DIM)
    pe = cmp_pe.reshape(2, 2, 1, chunk * HEAD_DIM)
    cblk = _compress(xc, pe, cmp_w1.astype(MXU_DTYPE), cmp_w2.astype(MXU_DTYPE), k_norm_g[0].reshape(1, HEAD_DIM))

    slopes = jnp.exp2(-8.0 * jnp.arange(1, heads + 1, dtype=F32) / heads)
    o_cmp, sel = _cmp_select(proj, cblk, slopes, batch, seq, tq=256)
    o = _nsa_attention(proj, sel, o_cmp, gl, gb, slopes, batch, seq, tq=256)
    return _matmul(o, w_out.astype(MXU_DTYPE), residual=h, out_dtype=F32, tm=1024, tn=512)


def kernel(x, sb_norm_g, sb_w_qkv, sb_w_out, nsa_norm_g, nsa_w_in, nsa_gate_b, nsa_q_norm_g, nsa_k_norm_g, nsa_cmp_pe, nsa_cmp_w1, nsa_cmp_w2, nsa_w_out, mlp_norm_g, mlp_w1, mlp_w2):
    batch, seq, d = x.shape
    depth = mlp_w1.shape[0]
    h = x.reshape(batch * seq, d)
    for layer in range(depth):
        slot = layer // 2
        if layer % 2 == 0:
            h = _sb_layer(h, batch, seq, sb_norm_g[slot], sb_w_qkv[slot], sb_w_out[slot])
        else:
            h = _nsa_layer(h, batch, seq, nsa_norm_g[slot], nsa_w_in[slot], nsa_gate_b[slot], nsa_q_norm_g[slot],
                           nsa_k_norm_g[slot], nsa_cmp_pe[slot], nsa_cmp_w1[slot], nsa_cmp_w2[slot], nsa_w_out[slot])
        h = _mlp(h, mlp_norm_g[layer], mlp_w1[layer].astype(MXU_DTYPE), mlp_w2[layer].astype(MXU_DTYPE), tm=512, tf=512)
    return h.reshape(batch, seq, d)
```
